```python
import math
import jax, jax.numpy as jnp
from jax import lax
import numpy as np

D_MODEL = 1024
BATCH = 8
SEQ = 8192
DEPTH = 2

GRID_W = 64
CTX_LEN = 256
N_MIXERS = 2
N_MOD = 9
D_FF = 2816
D_RNN = D_MODEL
LRU_HEADS = 8
LRU_BLOCK = D_RNN // LRU_HEADS
CONV_W = 4
CONV_LEFT = 2
RG_C = 8.0
CHUNK = 128
D_SGU = 2 * D_MODEL
SGU_GROUPS = 8
SGU_GROUP_W = D_SGU // SGU_GROUPS
EPS = 1e-6
POS_BASE = 10000.0

kernel_name = "hybrid_rglru_chunk_sgu_diffusion_trunk"


def rmsnorm(x, g):
    xf = x.astype(jnp.float32)
    y = xf * lax.rsqrt(jnp.mean(xf * xf, axis=-1, keepdims=True) + EPS)
    return y.astype(x.dtype) * g


def layernorm(x, g, b):
    xf = x.astype(jnp.float32)
    mu = jnp.mean(xf, axis=-1, keepdims=True)
    var = jnp.mean(jnp.square(xf - mu), axis=-1, keepdims=True)
    return ((xf - mu) * lax.rsqrt(var + EPS)).astype(x.dtype) * g + b


def sincos_1d(pos, dim):
    half = dim // 2
    omega = 1.0 / (POS_BASE ** (jnp.arange(half, dtype=jnp.float32) / half))
    ang = pos.astype(jnp.float32)[:, None] * omega[None, :]
    return jnp.concatenate([jnp.sin(ang), jnp.cos(ang)], axis=-1)


def sincos_2d(n_tokens, dim):
    rows = n_tokens // GRID_W
    emb_r = sincos_1d(jnp.arange(rows), dim // 2)
    emb_c = sincos_1d(jnp.arange(GRID_W), dim // 2)
    pe = jnp.concatenate([jnp.broadcast_to(emb_r[:, None, :], (rows, GRID_W, dim // 2)),
                          jnp.broadcast_to(emb_c[None, :, :], (rows, GRID_W, dim // 2))], axis=-1)
    return pe.reshape(rows * GRID_W, dim)


def swiglu(h, w1, w3, w2):
    return (jax.nn.silu(h @ w1) * (h @ w3)) @ w2


def centred_dwconv(x, w, b):
    L = x.shape[1]
    xp = jnp.pad(x, ((0, 0), (CONV_LEFT, CONV_W - 1 - CONV_LEFT), (0, 0)))
    y = xp[:, 0:L] * w[0]
    for k in range(1, CONV_W):
        y = y + xp[:, k:k + L] * w[k]
    return y + b


def _lin_combine(left, right):
    a_l, b_l = left
    a_r, b_r = right
    return a_l * a_r, a_r * b_l + b_r


def rglru_scan(x, wa, ba, wi, bi, lam, h0, reverse):
    B_, L, R = x.shape
    xf = x.astype(jnp.float32)
    xh = xf.reshape(B_, L, LRU_HEADS, LRU_BLOCK)
    r = jax.nn.sigmoid(jnp.einsum('blhi,hij->blhj', xh, wa.astype(jnp.float32)).reshape(B_, L, R) + ba.astype(jnp.float32))
    ig = jax.nn.sigmoid(jnp.einsum('blhi,hij->blhj', xh, wi.astype(jnp.float32)).reshape(B_, L, R) + bi.astype(jnp.float32))
    log_a = -RG_C * r * jax.nn.softplus(-lam.astype(jnp.float32))
    a = jnp.exp(log_a)
    b = jnp.sqrt(-jnp.expm1(2.0 * log_a)) * (ig * xf)
    if reverse:
        a = jnp.flip(a, axis=1)
        b = jnp.flip(b, axis=1)
    b = b.at[:, 0].add(a[:, 0] * h0)
    _, hs = lax.associative_scan(_lin_combine, (a, b), axis=1)
    if reverse:
        hs = jnp.flip(hs, axis=1)
    return hs


def rglru_block(h, w_in, conv_w, conv_b, wa, ba, wi, bi, lam, w_out, h0_f, h0_b):
    z = h @ w_in
    gate_br = jax.nn.gelu(z[..., :D_RNN], approximate=True)
    xc = centred_dwconv(z[..., D_RNN:], conv_w, conv_b)
    hf = rglru_scan(xc, wa[0], ba[0], wi[0], bi[0], lam[0], h0_f, reverse=False)
    hb = rglru_scan(xc, wa[1], ba[1], wi[1], bi[1], lam[1], h0_b, reverse=True)
    y = ((hf + hb).astype(gate_br.dtype) * gate_br) @ w_out
    return y, hf[:, -1], hb[:, 0]


def chunk_sgu(h, w_in, b_in, ln_g, ln_b, ws, bs, w_out):
    B_, L, _ = h.shape
    z = jax.nn.gelu(h @ w_in + b_in, approximate=True)
    u = z[..., :D_SGU]
    v = layernorm(z[..., D_SGU:], ln_g, ln_b)
    v = v.reshape(B_, L // CHUNK, CHUNK, SGU_GROUPS, SGU_GROUP_W)
    s = jnp.einsum('bnpgc,gqp->bnqgc', v, ws) + jnp.transpose(bs)[:, :, None]
    return (u * s.reshape(B_, L, D_SGU)) @ w_out


def setup_inputs(seed: int = 0) -> dict:
    key = jax.random.key(seed)
    ks = iter(jax.random.split(key, 40))
    f32 = jnp.float32

    def nrm(shape, s):
        return jax.random.normal(next(ks), shape, f32) * s

    n_a = len([i for i in range(DEPTH) if i % N_MIXERS == 0])
    n_b = len([i for i in range(DEPTH) if i % N_MIXERS == 1])
    u = jax.random.uniform(next(ks), (n_a, 2, D_RNN), f32, minval=0.9, maxval=0.999)
    a0 = u ** (1.0 / RG_C)
    lam = jnp.log(a0) - jnp.log1p(-a0)
    return {
        "x": nrm((BATCH, SEQ, D_MODEL), 1.0),
        "c": nrm((BATCH, D_MODEL), 1.0),
        "ctx": nrm((BATCH, CTX_LEN, D_MODEL), 1.0),
        "c_ctx": nrm((D_MODEL,), 1.0),
        "ada_w": nrm((DEPTH, D_MODEL, N_MOD * D_MODEL), 0.5 * D_MODEL ** -0.5),
        "ada_b": nrm((DEPTH, N_MOD * D_MODEL), 0.01),
        "norm_pre": 1.0 + nrm((DEPTH, 3, D_MODEL), 0.02),
        "norm_post": 1.0 + nrm((DEPTH, 3, D_MODEL), 0.02),
        "ffn_w1": nrm((DEPTH, 2, D_MODEL, D_FF), D_MODEL ** -0.5),
        "ffn_w3": nrm((DEPTH, 2, D_MODEL, D_FF), D_MODEL ** -0.5),
        "ffn_w2": nrm((DEPTH, 2, D_FF, D_MODEL), D_FF ** -0.5),
        "lru_w_in": nrm((n_a, D_MODEL, 2 * D_RNN), D_MODEL ** -0.5),
        "lru_conv_w": nrm((n_a, CONV_W, D_RNN), CONV_W ** -0.5),
        "lru_conv_b": nrm((n_a, D_RNN), 0.01),
        "lru_wa": nrm((n_a, 2, LRU_HEADS, LRU_BLOCK, LRU_BLOCK), LRU_BLOCK ** -0.5),
        "lru_ba": nrm((n_a, 2, D_RNN), 0.01),
        "lru_wi": nrm((n_a, 2, LRU_HEADS, LRU_BLOCK, LRU_BLOCK), LRU_BLOCK ** -0.5),
        "lru_bi": nrm((n_a, 2, D_RNN), 0.01),
        "lru_lambda": lam,
        "lru_w_out": nrm((n_a, D_RNN, D_MODEL), D_RNN ** -0.5),
        "sgu_w_in": nrm((n_b, D_MODEL, 2 * D_SGU), D_MODEL ** -0.5),
        "sgu_b_in": nrm((n_b, 2 * D_SGU), 0.01),
        "sgu_ln_g": 1.0 + nrm((n_b, D_SGU), 0.02),
        "sgu_ln_b": nrm((n_b, D_SGU), 0.01),
        "sgu_ws": nrm((n_b, SGU_GROUPS, CHUNK, CHUNK), CHUNK ** -0.5),
        "sgu_bs": 1.0 + nrm((n_b, SGU_GROUPS, CHUNK), 0.02),
        "sgu_w_out": nrm((n_b, D_SGU, D_MODEL), D_SGU ** -0.5),
    }


def reference(x, c, ctx, c_ctx, ada_w, ada_b, norm_pre, norm_post, ffn_w1, ffn_w3, ffn_w2,
              lru_w_in, lru_conv_w, lru_conv_b, lru_wa, lru_ba, lru_wi, lru_bi, lru_lambda, lru_w_out,
              sgu_w_in, sgu_b_in, sgu_ln_g, sgu_ln_b, sgu_ws, sgu_bs, sgu_w_out):
    n_lat = x.shape[1]
    x_lat = x + sincos_2d(n_lat, D_MODEL).astype(x.dtype)
    x_ctx = ctx

    def pre(xs, i, k, m):
        return rmsnorm(xs, norm_pre[i, k]) * (1.0 + m[3 * k + 1]) + m[3 * k]

    def post(xs, y, i, k, m, w):
        return xs + w * m[3 * k + 2] * rmsnorm(y, norm_post[i, k])

    def ffn_sub(xs, i, k, j, m):
        y = swiglu(pre(xs, i, k, m), ffn_w1[i, j], ffn_w3[i, j], ffn_w2[i, j])
        return post(xs, y, i, k, m, 0.5)

    for i in range(DEPTH):
        mixer = i % N_MIXERS
        mi = i // N_MIXERS
        need_ctx = any(j % N_MIXERS == 0 for j in range(i, DEPTH))
        ml = (jax.nn.silu(c) @ ada_w[i] + ada_b[i]).reshape(c.shape[0], N_MOD, D_MODEL)
        m_lat = [ml[:, k, None, :] for k in range(N_MOD)]
        mc = (jax.nn.silu(c_ctx) @ ada_w[i] + ada_b[i]).reshape(N_MOD, D_MODEL)
        m_ctx = [mc[k] for k in range(N_MOD)]

        x_lat = ffn_sub(x_lat, i, 0, 0, m_lat)
        if need_ctx:
            x_ctx = ffn_sub(x_ctx, i, 0, 0, m_ctx)

        if mixer == 0:
            lru_args = (lru_w_in[mi], lru_conv_w[mi], lru_conv_b[mi], lru_wa[mi], lru_ba[mi],
                        lru_wi[mi], lru_bi[mi], lru_lambda[mi], lru_w_out[mi])
            zeros = jnp.zeros((x_ctx.shape[0], D_RNN), jnp.float32)
            yc, hc_f, hc_b = rglru_block(pre(x_ctx, i, 1, m_ctx), *lru_args, zeros, zeros)
            yl, _, _ = rglru_block(pre(x_lat, i, 1, m_lat), *lru_args, hc_f, hc_b)
            x_ctx = post(x_ctx, yc, i, 1, m_ctx, 1.0)
            x_lat = post(x_lat, yl, i, 1, m_lat, 1.0)
        else:
            sgu_args = (sgu_w_in[mi], sgu_b_in[mi], sgu_ln_g[mi], sgu_ln_b[mi], sgu_ws[mi], sgu_bs[mi], sgu_w_out[mi])
            x_lat = post(x_lat, chunk_sgu(pre(x_lat, i, 1, m_lat), *sgu_args), i, 1, m_lat, 1.0)
            if need_ctx:
                x_ctx = post(x_ctx, chunk_sgu(pre(x_ctx, i, 1, m_ctx), *sgu_args), i, 1, m_ctx, 1.0)

        x_lat = ffn_sub(x_lat, i, 2, 1, m_lat)
        if need_ctx and any(j % N_MIXERS == 0 for j in range(i + 1, DEPTH)):
            x_ctx = ffn_sub(x_ctx, i, 2, 1, m_ctx)
    return x_lat
```

```python
import functools

import jax
import jax.numpy as jnp
from jax import lax
from jax.experimental import pallas as pl
from jax.experimental.pallas import tpu as pltpu

F32 = jnp.float32
BF16 = jnp.bfloat16

GRID_W = 64
N_MOD = 9
LRU_HEADS = 8
CONV_W = 4
RG_C = 8.0
CHUNK = 128
SGU_GROUPS = 8
EPS = 1e-6
POS_BASE = 10000.0

SUBLANES = 8
LANES = 128
MXU_DIM = 256
VMEM_LIMIT_BYTES = 56 * 1024 * 1024

FFN_ROWS = 512
FFN_COLS = MXU_DIM
LRU_STEPS = 64
SGU_ROWS = 512


def _sigmoid(x):
    return 0.5 * (jnp.tanh(0.5 * x) + 1.0)


def _gelu_tanh(x):
    return 0.5 * x * (1.0 + jnp.tanh(0.7978845608028654 * (x + 0.044715 * (x * x * x))))


def _rms(x):
    return x * lax.rsqrt(jnp.mean(x * x, axis=-1, keepdims=True) + EPS)


def _dot(a, b):
    return jnp.dot(a, b, preferred_element_type=F32)


def _params(*sem):
    return pltpu.CompilerParams(dimension_semantics=sem, vmem_limit_bytes=VMEM_LIMIT_BYTES)


def _resident(shape):
    nd = len(shape)
    return pl.BlockSpec(shape, lambda *_: (0,) * nd, pipeline_mode=pl.Buffered(1))


def _ada_kernel(c_ref, w_ref, b_ref, o_ref):
    cs = c_ref[...]
    s = cs * _sigmoid(cs)
    o_ref[...] = jnp.dot(s, w_ref[...], preferred_element_type=F32,
                         precision=lax.Precision.HIGHEST) + b_ref[...]


def _ada_call(c_all, ada_w, ada_b):
    depth, d, nd = ada_w.shape
    rows = c_all.shape[0]
    tn = d
    return pl.pallas_call(
        _ada_kernel,
        grid=(depth, nd // tn),
        in_specs=[
            pl.BlockSpec((rows, d), lambda l, n: (0, 0)),
            pl.BlockSpec((None, d, tn), lambda l, n: (l, 0, n)),
            pl.BlockSpec((None, 1, tn), lambda l, n: (l, 0, n)),
        ],
        out_specs=pl.BlockSpec((None, rows, tn), lambda l, n: (l, 0, n)),
        out_shape=jax.ShapeDtypeStruct((depth, rows, nd), F32),
        compiler_params=_params("parallel", "parallel"),
        name="ada_mod",
    )(c_all, ada_w, ada_b.reshape(depth, 1, nd))


def _ffn_kernel(*refs, k, nf, has_pe):
    if has_pe:
        x_ref, pe_ref, mod_ref, gpre_ref, gpost_ref, w1_ref, w3_ref, w2_ref, o_ref, h_s, acc_s = refs
        x = x_ref[...] + pe_ref[...]
    else:
        x_ref, mod_ref, gpre_ref, gpost_ref, w1_ref, w3_ref, w2_ref, o_ref, h_s, acc_s = refs
        x = x_ref[...]
    shift = mod_ref[3 * k:3 * k + 1, :]
    scale = mod_ref[3 * k + 1:3 * k + 2, :]
    gate = mod_ref[3 * k + 2:3 * k + 3, :]
    h = _rms(x) * gpre_ref[...] * (1.0 + scale) + shift
    h_s[...] = h.astype(BF16)
    for j in range(nf):
        hb = h_s[...]
        a = _dot(hb, w1_ref[j])
        b = _dot(hb, w3_ref[j])
        g = (a * _sigmoid(a) * b).astype(BF16)
        contrib = _dot(g, w2_ref[j])
        if j == 0:
            acc_s[...] = contrib
        else:
            acc_s[...] += contrib
    y = acc_s[...]
    o_ref[...] = x + (0.5 * gate) * (_rms(y) * gpost_ref[...])


def _stream_spec(tm, d, time_major):
    if time_major:
        return pl.BlockSpec((tm, d), lambda i, b: (i, b))
    return pl.BlockSpec((None, tm, d), lambda i, b: (b, i, 0))


def _ffn_call(x, pe, mod, gpre, gpost, w1, w3, w2, *, k, in_tm, out_tm, seq, name):
    nb, _, d = mod.shape
    dff = w1.shape[1]
    tf = FFN_COLS
    nf = dff // tf
    tm = min(FFN_ROWS, seq)
    w1c = w1.astype(BF16).reshape(d, nf, tf).transpose(1, 0, 2)
    w3c = w3.astype(BF16).reshape(d, nf, tf).transpose(1, 0, 2)
    w2c = w2.astype(BF16).reshape(nf, tf, d)
    in_specs = [_stream_spec(tm, d, in_tm)]
    args = [x]
    if pe is not None:
        in_specs.append(pl.BlockSpec((tm, d), lambda i, b: (i, 0)))
        args.append(pe)
    in_specs += [
        pl.BlockSpec((None, N_MOD, d), lambda i, b: (b, 0, 0)),
        _resident((1, d)),
        _resident((1, d)),
        _resident((nf, d, tf)),
        _resident((nf, d, tf)),
        _resident((nf, tf, d)),
    ]
    args += [mod, gpre.reshape(1, d), gpost.reshape(1, d), w1c, w3c, w2c]
    out_shape = (seq, nb * d) if out_tm else (nb, seq, d)
    return pl.pallas_call(
        functools.partial(_ffn_kernel, k=k, nf=nf, has_pe=pe is not None),
        grid=(seq // tm, nb),
        in_specs=in_specs,
        out_specs=_stream_spec(tm, d, out_tm),
        out_shape=jax.ShapeDtypeStruct(out_shape, F32),
        scratch_shapes=[pltpu.VMEM((tm, d), BF16), pltpu.VMEM((tm, d), F32)],
        compiler_params=_params("parallel", "parallel"),
        name=name,
    )(*args)


def _lru_gates(xc, wg_ref, bg_ref, lam_ref, a_s, b_s):
    xcb = xc.astype(BF16)
    lam = lam_ref[...]
    neg = -lam
    sp = jnp.maximum(neg, 0.0) + jnp.log1p(jnp.exp(-jnp.abs(neg)))
    cdec = (-RG_C) * sp
    hw = LANES
    for hd in range(LRU_HEADS):
        cols = slice(hd * hw, (hd + 1) * hw)
        pre = _dot(xcb[:, cols], wg_ref[hd])
        r = _sigmoid(pre[:, :hw] + bg_ref[0:1, cols])
        ig = _sigmoid(pre[:, hw:] + bg_ref[1:2, cols])
        log_a = cdec[:, cols] * r
        a = jnp.exp(log_a)
        one_minus_a2 = -jnp.tanh(log_a) * (a * a + 1.0)
        a_s[:, cols] = a
        b_s[:, cols] = jnp.sqrt(one_minus_a2) * (ig * xc[:, cols])


def _lru_fwd_kernel(x_ref, xh_ref, mod_ref, gpre_ref, win_ref, cw_ref, cb_ref, wg_ref, bg_ref,
                    lam_ref, h0_ref, gate_ref, xc_ref, hf_ref, hlast_ref,
                    lhs_s, u_s, a_s, b_s, hc_s, *, tt, nb, halo):
    i = pl.program_id(0)
    last = pl.num_programs(0) - 1
    rows = tt * nb
    d = x_ref.shape[-1]

    @pl.when(i == 0)
    def _():
        u_s[0:2 * nb, :] = jnp.zeros((2 * nb, d), F32)
        hc_s[...] = h0_ref[...]

    shift = mod_ref[3]
    scale = mod_ref[4]

    def pre(xr, n):
        x3 = xr.reshape(n, nb, d)
        h3 = _rms(x3) * gpre_ref[...] * (1.0 + scale) + shift
        return h3.reshape(n * nb, d).astype(BF16)

    lhs_s[0:rows, :] = pre(x_ref[...], tt)
    lhs_s[rows:rows + halo, :] = pre(xh_ref[...], halo // nb)
    gate_ref[...] = _gelu_tanh(_dot(lhs_s[0:rows, :], win_ref[:, 0:d])).astype(BF16)
    u_s[2 * nb:2 * nb + rows + halo, :] = _dot(lhs_s[...], win_ref[:, d:2 * d])

    @pl.when(i == last)
    def _():
        u_s[2 * nb + rows:2 * nb + rows + halo, :] = jnp.zeros((halo, d), F32)

    xc = cb_ref[...] + cw_ref[0:1, :] * u_s[0:rows, :]
    for tap in range(1, CONV_W):
        xc = xc + cw_ref[tap:tap + 1, :] * u_s[tap * nb:tap * nb + rows, :]
    u_s[0:2 * nb, :] = u_s[rows:rows + 2 * nb, :]
    xc_ref[...] = xc
    _lru_gates(xc, wg_ref, bg_ref, lam_ref, a_s, b_s)

    def step(t, h):
        r0 = pl.multiple_of(t * nb, nb)
        h = a_s[pl.ds(r0, nb), :] * h + b_s[pl.ds(r0, nb), :]
        hf_ref[pl.ds(r0, nb), :] = h
        return h

    h = lax.fori_loop(0, tt, step, hc_s[...], unroll=8)
    hc_s[...] = h
    hlast_ref[...] = h


def _lru_bwd_kernel(xc_ref, hf_ref, gate_ref, x_ref, mod_ref, gpost_ref, wg_ref, bg_ref, lam_ref,
                    wout_ref, h0_ref, o_ref, hfirst_ref, a_s, b_s, hs_s, hc_s, *, tt, nb):
    i = pl.program_id(0)
    rows = tt * nb
    d = x_ref.shape[-1]

    @pl.when(i == 0)
    def _():
        hc_s[...] = h0_ref[...]

    _lru_gates(xc_ref[...], wg_ref, bg_ref, lam_ref, a_s, b_s)

    def step(s, h):
        r0 = pl.multiple_of((tt - 1 - s) * nb, nb)
        h = a_s[pl.ds(r0, nb), :] * h + b_s[pl.ds(r0, nb), :]
        hs_s[pl.ds(r0, nb), :] = h + hf_ref[pl.ds(r0, nb), :]
        return h

    h = lax.fori_loop(0, tt, step, hc_s[...], unroll=8)
    hc_s[...] = h
    hfirst_ref[...] = h

    m = (hs_s[...] * gate_ref[...].astype(F32)).astype(BF16)
    y = _dot(m, wout_ref[...])
    y3 = _rms(y.reshape(tt, nb, d)) * gpost_ref[...]
    o_ref[...] = x_ref[...] + (mod_ref[5] * y3).reshape(rows, d)


def _lru_call(x_tm, modT, gpre, gpost, w_in, conv_w, conv_b, wa, ba, wi, bi, lam, w_out,
              h0_f, h0_b, *, seq, name):
    nb, d = h0_f.shape
    tt = min(LRU_STEPS, seq)
    rows = tt * nb
    nt = seq // tt
    halo = 2 * nb
    x2d = x_tm.reshape(seq * nb, d)
    w_in_b = w_in.astype(BF16)
    wg = jnp.concatenate([wa, wi], axis=-1).astype(BF16)
    bg = jnp.stack([ba, bi], axis=1)
    n_halo_blocks = seq * nb // halo
    row_spec = pl.BlockSpec((rows, d), lambda i: (i, 0))
    state_spec = pl.BlockSpec((nb, d), lambda i: (0, 0))

    gate, xc, hf, h_last = pl.pallas_call(
        functools.partial(_lru_fwd_kernel, tt=tt, nb=nb, halo=halo),
        grid=(nt,),
        in_specs=[
            row_spec,
            pl.BlockSpec((halo, d), lambda i: (jnp.minimum((i + 1) * (rows // halo), n_halo_blocks - 1), 0)),
            _resident((N_MOD, nb, d)),
            _resident((1, d)),
            _resident((d, 2 * d)),
            _resident((CONV_W, d)),
            _resident((1, d)),
            _resident((LRU_HEADS, d // LRU_HEADS, 2 * d // LRU_HEADS)),
            _resident((2, d)),
            _resident((1, d)),
            state_spec,
        ],
        out_specs=[row_spec, row_spec, row_spec, state_spec],
        out_shape=[
            jax.ShapeDtypeStruct((seq * nb, d), BF16),
            jax.ShapeDtypeStruct((seq * nb, d), F32),
            jax.ShapeDtypeStruct((seq * nb, d), F32),
            jax.ShapeDtypeStruct((nb, d), F32),
        ],
        scratch_shapes=[
            pltpu.VMEM((rows + halo, d), BF16),
            pltpu.VMEM((rows + 2 * halo, d), F32),
            pltpu.VMEM((rows, d), F32),
            pltpu.VMEM((rows, d), F32),
            pltpu.VMEM((nb, d), F32),
        ],
        compiler_params=_params("arbitrary"),
        name=name + "_fwd",
    )(x2d, x2d, modT, gpre.reshape(1, d), w_in_b, conv_w, conv_b.reshape(1, d),
      wg[0], bg[0], lam[0].reshape(1, d), h0_f)

    rev_spec = pl.BlockSpec((rows, d), lambda i: (nt - 1 - i, 0))
    x_new, h_first = pl.pallas_call(
        functools.partial(_lru_bwd_kernel, tt=tt, nb=nb),
        grid=(nt,),
        in_specs=[
            rev_spec, rev_spec, rev_spec, rev_spec,
            _resident((N_MOD, nb, d)),
            _resident((1, d)),
            _resident((LRU_HEADS, d // LRU_HEADS, 2 * d // LRU_HEADS)),
            _resident((2, d)),
            _resident((1, d)),
            _resident((d, d)),
            state_spec,
        ],
        out_specs=[rev_spec, state_spec],
        out_shape=[
            jax.ShapeDtypeStruct((seq * nb, d), F32),
            jax.ShapeDtypeStruct((nb, d), F32),
        ],
        scratch_shapes=[
            pltpu.VMEM((rows, d), F32),
            pltpu.VMEM((rows, d), F32),
            pltpu.VMEM((rows, d), F32),
            pltpu.VMEM((nb, d), F32),
        ],
        compiler_params=_params("arbitrary"),
        name=name + "_bwd",
    )(xc, hf, gate, x2d, modT, gpost.reshape(1, d), wg[1], bg[1], lam[1].reshape(1, d),
      w_out.astype(BF16), h0_b)
    return x_new.reshape(seq, nb * d), h_last, h_first


def _sgu_kernel(x_ref, mod_ref, gpre_ref, gpost_ref, wu_ref, bu_ref, wv_ref, bv_ref, lng_ref,
                lnb_ref, ws_ref, bs_ref, wo_ref, o_ref, h_s, v_s, acc_s, *, tm):
    x = x_ref[...]
    h = _rms(x) * gpre_ref[...] * (1.0 + mod_ref[4:5, :]) + mod_ref[3:4, :]
    h_s[...] = h.astype(BF16)
    vp = _gelu_tanh(_dot(h_s[...], wv_ref[...]) + bv_ref[...])
    mu = jnp.mean(vp, axis=-1, keepdims=True)
    vc = vp - mu
    var = jnp.mean(vc * vc, axis=-1, keepdims=True)
    v_s[...] = ((vc * lax.rsqrt(var + EPS)) * lng_ref[...] + lnb_ref[...]).astype(BF16)
    gw = wu_ref.shape[-1]
    for g in range(SGU_GROUPS):
        ug = _gelu_tanh(_dot(h_s[...], wu_ref[g]) + bu_ref[g])
        parts = []
        for n in range(tm // CHUNK):
            rs = slice(n * CHUNK, (n + 1) * CHUNK)
            s = _dot(ws_ref[g], v_s[rs, g * gw:(g + 1) * gw]) + bs_ref[g]
            parts.append(ug[rs, :] * s)
        prod = jnp.concatenate(parts, axis=0).astype(BF16)
        contrib = _dot(prod, wo_ref[g])
        if g == 0:
            acc_s[...] = contrib
        else:
            acc_s[...] += contrib
    y = acc_s[...]
    o_ref[...] = x + mod_ref[5:6, :] * (_rms(y) * gpost_ref[...])


def _sgu_call(x_tm, mod, gpre, gpost, w_in, b_in, ln_g, ln_b, ws, bs, w_out, *, seq, name):
    nb, _, d = mod.shape
    dsgu = w_out.shape[0]
    gw = dsgu // SGU_GROUPS
    tm = SGU_ROWS
    wu = w_in[:, :dsgu].astype(BF16).reshape(d, SGU_GROUPS, gw).transpose(1, 0, 2)
    wv = w_in[:, dsgu:].astype(BF16)
    bu = b_in[:dsgu].reshape(SGU_GROUPS, 1, gw)
    bv = b_in[dsgu:].reshape(1, dsgu)
    bsb = jnp.broadcast_to(bs[:, :, None], (SGU_GROUPS, CHUNK, gw))
    wo = w_out.astype(BF16).reshape(SGU_GROUPS, gw, d)
    stream = _stream_spec(tm, d, True)
    return pl.pallas_call(
        functools.partial(_sgu_kernel, tm=tm),
        grid=(seq // tm, nb),
        in_specs=[
            stream,
            pl.BlockSpec((None, N_MOD, d), lambda i, b: (b, 0, 0)),
            _resident((1, d)),
            _resident((1, d)),
            _resident((SGU_GROUPS, d, gw)),
            _resident((SGU_GROUPS, 1, gw)),
            _resident((d, dsgu)),
            _resident((1, dsgu)),
            _resident((1, dsgu)),
            _resident((1, dsgu)),
            _resident((SGU_GROUPS, CHUNK, CHUNK)),
            _resident((SGU_GROUPS, CHUNK, gw)),
            _resident((SGU_GROUPS, gw, d)),
        ],
        out_specs=stream,
        out_shape=jax.ShapeDtypeStruct((seq, nb * d), F32),
        scratch_shapes=[
            pltpu.VMEM((tm, d), BF16),
            pltpu.VMEM((tm, dsgu), BF16),
            pltpu.VMEM((tm, d), F32),
        ],
        compiler_params=_params("parallel", "parallel"),
        name=name,
    )(x_tm, mod, gpre.reshape(1, d), gpost.reshape(1, d), wu, bu, wv, bv,
      ln_g.reshape(1, dsgu), ln_b.reshape(1, dsgu), ws.astype(BF16), bsb, wo)


def _sincos_1d(pos, dim):
    half = dim // 2
    omega = 1.0 / (POS_BASE ** (jnp.arange(half, dtype=F32) / half))
    ang = pos.astype(F32)[:, None] * omega[None, :]
    return jnp.concatenate([jnp.sin(ang), jnp.cos(ang)], axis=-1)


def _sincos_2d(n_tokens, dim):
    rows = n_tokens // GRID_W
    emb_r = _sincos_1d(jnp.arange(rows), dim // 2)
    emb_c = _sincos_1d(jnp.arange(GRID_W), dim // 2)
    pe = jnp.concatenate([jnp.broadcast_to(emb_r[:, None, :], (rows, GRID_W, dim // 2)),
                          jnp.broadcast_to(emb_c[None, :, :], (rows, GRID_W, dim // 2))], axis=-1)
    return pe.reshape(rows * GRID_W, dim)


def kernel(x, c, ctx, c_ctx, ada_w, ada_b, norm_pre, norm_post, ffn_w1, ffn_w3, ffn_w2, lru_w_in, lru_conv_w, lru_conv_b, lru_wa, lru_ba, lru_wi, lru_bi, lru_lambda, lru_w_out, sgu_w_in, sgu_b_in, sgu_ln_g, sgu_ln_b, sgu_ws, sgu_bs, sgu_w_out):
    nb, seq, d = x.shape
    n_ctx = ctx.shape[1]
    depth = ada_w.shape[0]
    assert depth == 2 and nb == SUBLANES

    pe = _sincos_2d(seq, d).astype(x.dtype)
    pad = jnp.zeros((2 * SUBLANES - nb - 1, d), F32)
    c_all = jnp.concatenate([c, c_ctx[None, :], pad], axis=0)
    mods = _ada_call(c_all, ada_w, ada_b).reshape(depth, 2 * SUBLANES, N_MOD, d)
    m_lat = [mods[i, :nb] for i in range(depth)]
    m_ctx = jnp.broadcast_to(mods[0, nb][None], (nb, N_MOD, d))

    def ffn(xs, pe_, mod, i, k, j, in_tm, out_tm, s, name):
        return _ffn_call(xs, pe_, mod, norm_pre[i, k], norm_post[i, k], ffn_w1[i, j], ffn_w3[i, j],
                         ffn_w2[i, j], k=k, in_tm=in_tm, out_tm=out_tm, seq=s, name=name)

    def lru(xs, mod, h0f, h0b, s, name):
        return _lru_call(xs, mod.transpose(1, 0, 2), norm_pre[0, 1], norm_post[0, 1], lru_w_in[0],
                         lru_conv_w[0], lru_conv_b[0], lru_wa[0], lru_ba[0], lru_wi[0], lru_bi[0],
                         lru_lambda[0], lru_w_out[0], h0f, h0b, seq=s, name=name)

    x_lat = ffn(x, pe, m_lat[0], 0, 0, 0, False, True, seq, "l0_ffn0_lat")
    x_ctx = ffn(ctx, None, m_ctx, 0, 0, 0, False, True, n_ctx, "l0_ffn0_ctx")
    zeros = jnp.zeros((nb, d), F32)
    _, hc_f, hc_b = lru(x_ctx, m_ctx, zeros, zeros, n_ctx, "l0_lru_ctx")
    x_lat, _, _ = lru(x_lat, m_lat[0], hc_f, hc_b, seq, "l0_lru_lat")
    x_lat = ffn(x_lat, None, m_lat[0], 0, 2, 1, True, True, seq, "l0_ffn1_lat")

    x_lat = ffn(x_lat, None, m_lat[1], 1, 0, 0, True, True, seq, "l1_ffn0_lat")
    x_lat = _sgu_call(x_lat, m_lat[1], norm_pre[1, 1], norm_post[1, 1], sgu_w_in[0], sgu_b_in[0],
                      sgu_ln_g[0], sgu_ln_b[0], sgu_ws[0], sgu_bs[0], sgu_w_out[0], seq=seq,
                      name="l1_sgu_lat")
    return ffn(x_lat, None, m_lat[1], 1, 2, 1, True, False, seq, "l1_ffn1_lat")
```

```python
import functools

import jax
import jax.numpy as jnp
from jax import lax
from jax.experimental import pallas as pl
from jax.experimental.pallas import tpu as pltpu

F32 = jnp.float32
BF16 = jnp.bfloat16

GRID_W = 64
N_MOD = 9
LRU_HEADS = 8
CONV_W = 4
RG_C = 8.0
CHUNK = 128
SGU_GROUPS = 8
EPS = 1e-6
POS_BASE = 10000.0

SUBLANES = 8
LANES = 128
MXU_DIM = 256
BF16_ROWS = 2 * SUBLANES
VMEM_LIMIT_BYTES = 56 * 1024 * 1024

FFN_ROWS = 512
FFN_SPLIT = 1
FFN_COLS = MXU_DIM
LRU_STEPS = 64
SGU_ROWS = 512

GELU_K = 0.7978845608028654
GELU_C = 0.044715


def _sigmoid(x):
    return 0.5 * (jnp.tanh(0.5 * x) + 1.0)


def _gelu_tanh(x):
    hx = 0.5 * x
    return hx * jnp.tanh(x * ((GELU_K * GELU_C) * (x * x) + GELU_K)) + hx


def _sqrt_nonneg(x):
    return jnp.where(x > 0.0, x * lax.rsqrt(x), 0.0)


def _rms(x):
    return x * lax.rsqrt(jnp.mean(x * x, axis=-1, keepdims=True) + EPS)


def _dot(a, b):
    return jnp.dot(a, b, preferred_element_type=F32)


def _params(*sem):
    return pltpu.CompilerParams(dimension_semantics=sem, vmem_limit_bytes=VMEM_LIMIT_BYTES)


def _resident(shape):
    nd = len(shape)
    return pl.BlockSpec(shape, lambda *_: (0,) * nd, pipeline_mode=pl.Buffered(1))


def _ada_kernel(c_ref, w_ref, b_ref, o_ref):
    cs = c_ref[...]
    s = cs * _sigmoid(cs)
    o_ref[...] = jnp.dot(s, w_ref[...], preferred_element_type=F32,
                         precision=lax.Precision.HIGHEST) + b_ref[...]


def _ada_call(c_all, ada_w, ada_b):
    depth, d, nd = ada_w.shape
    rows = c_all.shape[0]
    tn = d
    return pl.pallas_call(
        _ada_kernel,
        grid=(depth, nd // tn),
        in_specs=[
            pl.BlockSpec((rows, d), lambda l, n: (0, 0)),
            pl.BlockSpec((None, d, tn), lambda l, n: (l, 0, n)),
            pl.BlockSpec((None, 1, tn), lambda l, n: (l, 0, n)),
        ],
        out_specs=pl.BlockSpec((None, rows, tn), lambda l, n: (l, 0, n)),
        out_shape=jax.ShapeDtypeStruct((depth, rows, nd), F32),
        compiler_params=_params("parallel", "parallel"),
        name="ada_mod",
    )(c_all, ada_w, ada_b.reshape(depth, 1, nd))


def _ffn_kernel(*refs, k, tf, split, has_pe):
    if has_pe:
        x_ref, pe_ref, mod_ref, gpre_ref, gpost_ref, w1_ref, w3_ref, w2_ref, o_ref, h_s, acc_s = refs
    else:
        x_ref, mod_ref, gpre_ref, gpost_ref, w1_ref, w3_ref, w2_ref, o_ref, h_s, acc_s = refs
        pe_ref = None
    shift = mod_ref[3 * k:3 * k + 1, :]
    pre_gain = gpre_ref[...] * (1.0 + mod_ref[3 * k + 1:3 * k + 2, :])
    post_gain = (0.5 * mod_ref[3 * k + 2:3 * k + 3, :]) * gpost_ref[...]
    tm = x_ref.shape[0]
    rows = tm // split
    nf = w1_ref.shape[1] // tf
    for s in range(split):
        rs = slice(s * rows, (s + 1) * rows)
        x = x_ref[rs, :]
        if pe_ref is not None:
            x = x + pe_ref[rs, :]
        h_s[rs, :] = (_rms(x) * pre_gain + shift).astype(BF16)
        for j in range(nf):
            cs = slice(j * tf, (j + 1) * tf)
            hb = h_s[rs, :]
            a = _dot(hb, w1_ref[:, cs])
            b = _dot(hb, w3_ref[:, cs])
            g = (a * _sigmoid(a) * b).astype(BF16)
            contrib = _dot(g, w2_ref[cs, :])
            if j == 0:
                acc_s[rs, :] = contrib
            else:
                acc_s[rs, :] += contrib
        o_ref[rs, :] = x + _rms(acc_s[rs, :]) * post_gain


def _ffn_call(x, pe, mod, gpre, gpost, w1, w3, w2, *, k, name):
    nb, seq, d = x.shape
    dff = w1.shape[1]
    tm = min(FFN_ROWS, seq)
    stream = pl.BlockSpec((None, tm, d), lambda i, b: (b, i, 0))
    in_specs = [stream]
    args = [x]
    if pe is not None:
        in_specs.append(pl.BlockSpec((tm, d), lambda i, b: (i, 0)))
        args.append(pe)
    in_specs += [
        pl.BlockSpec((None, N_MOD, d), lambda i, b: (b, 0, 0)),
        _resident((1, d)),
        _resident((1, d)),
        _resident((d, dff)),
        _resident((d, dff)),
        _resident((dff, d)),
    ]
    args += [mod, gpre.reshape(1, d), gpost.reshape(1, d),
             w1.astype(BF16), w3.astype(BF16), w2.astype(BF16)]
    return pl.pallas_call(
        functools.partial(_ffn_kernel, k=k, tf=FFN_COLS, split=FFN_SPLIT, has_pe=pe is not None),
        grid=(seq // tm, nb),
        in_specs=in_specs,
        out_specs=stream,
        out_shape=jax.ShapeDtypeStruct((nb, seq, d), F32),
        scratch_shapes=[pltpu.VMEM((tm, d), BF16), pltpu.VMEM((tm, d), F32)],
        compiler_params=_params("parallel", "parallel"),
        name=name,
    )(*args)


def _lru_gates(xc, wg_ref, bg_ref, lam_ref, a_s, b_s):
    xcb = xc.astype(BF16)
    neg = -lam_ref[...]
    softplus = jnp.maximum(neg, 0.0) + jnp.log1p(jnp.exp(-jnp.abs(neg)))
    cdec = (-RG_C) * softplus
    hw = LANES
    for hd in range(LRU_HEADS):
        cols = slice(hd * hw, (hd + 1) * hw)
        pre = _dot(xcb[:, cols], wg_ref[hd])
        r = _sigmoid(pre[:, :hw] + bg_ref[0:1, cols])
        ig = _sigmoid(pre[:, hw:] + bg_ref[1:2, cols])
        log_a = cdec[:, cols] * r
        a = jnp.exp(log_a)
        one_minus_a2 = jnp.tanh(log_a) * (-1.0 - a * a)
        a_s[:, cols] = a
        b_s[:, cols] = _sqrt_nonneg(one_minus_a2) * (ig * xc[:, cols])


def _lru_fwd_kernel(x_ref, xh_ref, mod_ref, gpre_ref, win_ref, cw_ref, cb_ref, wg_ref, bg_ref,
                    lam_ref, h0_ref, gate_ref, xc_ref, hf_ref, hlast_ref,
                    lhs_s, u_s, a_s, b_s, hc_s, *, tt, nb, halo):
    i = pl.program_id(0)
    last = pl.num_programs(0) - 1
    rows = tt * nb
    d = x_ref.shape[-1]

    @pl.when(i == 0)
    def _():
        u_s[0:2 * nb, :] = jnp.zeros((2 * nb, d), F32)
        hc_s[...] = h0_ref[...]

    shift = mod_ref[3]
    pre_gain = gpre_ref[...] * (1.0 + mod_ref[4])

    def pre(x_bt, steps):
        x3 = jnp.swapaxes(x_bt, 0, 1)[0:steps]
        h3 = _rms(x3) * pre_gain + shift
        return h3.reshape(steps * nb, d).astype(BF16)

    lhs_s[0:rows, :] = pre(x_ref[...], tt)
    lhs_s[rows:rows + halo, :] = pre(xh_ref[...], halo // nb)
    gate_ref[...] = _gelu_tanh(_dot(lhs_s[0:rows, :], win_ref[:, 0:d])).astype(BF16)
    u_s[2 * nb:2 * nb + rows + halo, :] = _dot(lhs_s[...], win_ref[:, d:2 * d])

    @pl.when(i == last)
    def _():
        u_s[2 * nb + rows:2 * nb + rows + halo, :] = jnp.zeros((halo, d), F32)

    xc = cb_ref[...] + cw_ref[0:1, :] * u_s[0:rows, :]
    for tap in range(1, CONV_W):
        xc = xc + cw_ref[tap:tap + 1, :] * u_s[tap * nb:tap * nb + rows, :]
    u_s[0:2 * nb, :] = u_s[rows:rows + 2 * nb, :]
    xc_ref[...] = xc
    _lru_gates(xc, wg_ref, bg_ref, lam_ref, a_s, b_s)

    def step(t, h):
        r0 = pl.multiple_of(t * nb, nb)
        h = a_s[pl.ds(r0, nb), :] * h + b_s[pl.ds(r0, nb), :]
        hf_ref[pl.ds(r0, nb), :] = h
        return h

    h = lax.fori_loop(0, tt, step, hc_s[...], unroll=8)
    hc_s[...] = h
    hlast_ref[...] = h


def _lru_bwd_kernel(xc_ref, hf_ref, gate_ref, x_ref, mod_ref, gpost_ref, wg_ref, bg_ref, lam_ref,
                    wout_ref, h0_ref, o_ref, hfirst_ref, a_s, b_s, hs_s, hc_s, *, tt, nb):
    i = pl.program_id(0)
    d = x_ref.shape[-1]

    @pl.when(i == 0)
    def _():
        hc_s[...] = h0_ref[...]

    _lru_gates(xc_ref[...], wg_ref, bg_ref, lam_ref, a_s, b_s)

    def step(s, h):
        r0 = pl.multiple_of((tt - 1 - s) * nb, nb)
        h = a_s[pl.ds(r0, nb), :] * h + b_s[pl.ds(r0, nb), :]
        hs_s[pl.ds(r0, nb), :] = h + hf_ref[pl.ds(r0, nb), :]
        return h

    h = lax.fori_loop(0, tt, step, hc_s[...], unroll=8)
    hc_s[...] = h
    hfirst_ref[...] = h

    m = (hs_s[...] * gate_ref[...].astype(F32)).astype(BF16)
    y3 = _dot(m, wout_ref[...]).reshape(tt, nb, d)
    upd = _rms(y3) * (mod_ref[5] * gpost_ref[...])
    o_ref[...] = x_ref[...] + jnp.swapaxes(upd, 0, 1)


def _lru_call(x, modT, gpre, gpost, w_in, conv_w, conv_b, wa, ba, wi, bi, lam, w_out,
              h0_f, h0_b, *, name):
    nb, seq, d = x.shape
    tt = min(LRU_STEPS, seq)
    rows = tt * nb
    nt = seq // tt
    halo = BF16_ROWS
    halo_t = SUBLANES
    hw = d // LRU_HEADS
    w_in_b = w_in.astype(BF16)
    wg = jnp.concatenate([wa, wi], axis=-1).astype(BF16)
    bg = jnp.stack([ba, bi], axis=1)
    n_halo_blocks = seq // halo_t
    row_spec = pl.BlockSpec((rows, d), lambda i: (i, 0))
    state_spec = pl.BlockSpec((nb, d), lambda i: (0, 0))

    gate, xc, hf, h_last = pl.pallas_call(
        functools.partial(_lru_fwd_kernel, tt=tt, nb=nb, halo=halo),
        grid=(nt,),
        in_specs=[
            pl.BlockSpec((nb, tt, d), lambda i: (0, i, 0)),
            pl.BlockSpec((nb, halo_t, d),
                         lambda i: (0, jnp.minimum((i + 1) * (tt // halo_t), n_halo_blocks - 1), 0)),
            _resident((N_MOD, nb, d)),
            _resident((1, d)),
            _resident((d, 2 * d)),
            _resident((CONV_W, d)),
            _resident((1, d)),
            _resident((LRU_HEADS, hw, 2 * hw)),
            _resident((2, d)),
            _resident((1, d)),
            state_spec,
        ],
        out_specs=[row_spec, row_spec, row_spec, state_spec],
        out_shape=[
            jax.ShapeDtypeStruct((seq * nb, d), BF16),
            jax.ShapeDtypeStruct((seq * nb, d), F32),
            jax.ShapeDtypeStruct((seq * nb, d), F32),
            jax.ShapeDtypeStruct((nb, d), F32),
        ],
        scratch_shapes=[
            pltpu.VMEM((rows + halo, d), BF16),
            pltpu.VMEM((rows + 2 * halo, d), F32),
            pltpu.VMEM((rows, d), F32),
            pltpu.VMEM((rows, d), F32),
            pltpu.VMEM((nb, d), F32),
        ],
        compiler_params=_params("arbitrary"),
        name=name + "_fwd",
    )(x, x, modT, gpre.reshape(1, d), w_in_b, conv_w, conv_b.reshape(1, d),
      wg[0], bg[0], lam[0].reshape(1, d), h0_f)

    rev_spec = pl.BlockSpec((rows, d), lambda i: (nt - 1 - i, 0))
    rev_stream = pl.BlockSpec((nb, tt, d), lambda i: (0, nt - 1 - i, 0))
    x_new, h_first = pl.pallas_call(
        functools.partial(_lru_bwd_kernel, tt=tt, nb=nb),
        grid=(nt,),
        in_specs=[
            rev_spec, rev_spec, rev_spec, rev_stream,
            _resident((N_MOD, nb, d)),
            _resident((1, d)),
            _resident((LRU_HEADS, hw, 2 * hw)),
            _resident((2, d)),
            _resident((1, d)),
            _resident((d, d)),
            state_spec,
        ],
        out_specs=[rev_stream, state_spec],
        out_shape=[
            jax.ShapeDtypeStruct((nb, seq, d), F32),
            jax.ShapeDtypeStruct((nb, d), F32),
        ],
        scratch_shapes=[
            pltpu.VMEM((rows, d), F32),
            pltpu.VMEM((rows, d), F32),
            pltpu.VMEM((rows, d), F32),
            pltpu.VMEM((nb, d), F32),
        ],
        compiler_params=_params("arbitrary"),
        name=name + "_bwd",
    )(xc, hf, gate, x, modT, gpost.reshape(1, d), wg[1], bg[1], lam[1].reshape(1, d),
      w_out.astype(BF16), h0_b)
    return x_new, h_last, h_first


def _sgu_kernel(x_ref, mod_ref, gpre_ref, gpost_ref, win_ref, bin_ref, lng_ref, lnb_ref, ws_ref,
                bs_ref, wo_ref, o_ref, h_s, v_s, p_s):
    x = x_ref[...]
    tm = x.shape[0]
    dsgu = v_s.shape[1]
    gw = dsgu // SGU_GROUPS
    h_s[...] = (_rms(x) * (gpre_ref[...] * (1.0 + mod_ref[4:5, :])) + mod_ref[3:4, :]).astype(BF16)
    vp = _gelu_tanh(_dot(h_s[...], win_ref[:, dsgu:2 * dsgu]) + bin_ref[:, dsgu:2 * dsgu])
    mu = jnp.mean(vp, axis=-1, keepdims=True)
    vc = vp - mu
    var = jnp.mean(vc * vc, axis=-1, keepdims=True)
    v_s[...] = ((vc * lax.rsqrt(var + EPS)) * lng_ref[...] + lnb_ref[...]).astype(BF16)
    for g in range(SGU_GROUPS):
        cs = slice(g * gw, (g + 1) * gw)
        ug = _gelu_tanh(_dot(h_s[...], win_ref[:, cs]) + bin_ref[:, cs])
        for n in range(tm // CHUNK):
            rs = slice(n * CHUNK, (n + 1) * CHUNK)
            s = _dot(ws_ref[g], v_s[rs, cs]) + bs_ref[g]
            p_s[rs, cs] = (ug[rs, :] * s).astype(BF16)
    y = _dot(p_s[...], wo_ref[...])
    o_ref[...] = x + _rms(y) * (mod_ref[5:6, :] * gpost_ref[...])


def _sgu_call(x, mod, gpre, gpost, w_in, b_in, ln_g, ln_b, ws, bs, w_out, *, name):
    nb, seq, d = x.shape
    dsgu = w_out.shape[0]
    gw = dsgu // SGU_GROUPS
    tm = SGU_ROWS
    bsb = jnp.broadcast_to(bs[:, :, None], (SGU_GROUPS, CHUNK, gw))
    stream = pl.BlockSpec((None, tm, d), lambda i, b: (b, i, 0))
    return pl.pallas_call(
        _sgu_kernel,
        grid=(seq // tm, nb),
        in_specs=[
            stream,
            pl.BlockSpec((None, N_MOD, d), lambda i, b: (b, 0, 0)),
            _resident((1, d)),
            _resident((1, d)),
            _resident((d, 2 * dsgu)),
            _resident((1, 2 * dsgu)),
            _resident((1, dsgu)),
            _resident((1, dsgu)),
            _resident((SGU_GROUPS, CHUNK, CHUNK)),
            _resident((SGU_GROUPS, CHUNK, gw)),
            _resident((dsgu, d)),
        ],
        out_specs=stream,
        out_shape=jax.ShapeDtypeStruct((nb, seq, d), F32),
        scratch_shapes=[
            pltpu.VMEM((tm, d), BF16),
            pltpu.VMEM((tm, dsgu), BF16),
            pltpu.VMEM((tm, dsgu), BF16),
        ],
        compiler_params=_params("parallel", "parallel"),
        name=name,
    )(x, mod, gpre.reshape(1, d), gpost.reshape(1, d), w_in.astype(BF16), b_in.reshape(1, 2 * dsgu),
      ln_g.reshape(1, dsgu), ln_b.reshape(1, dsgu), ws.astype(BF16), bsb, w_out.astype(BF16))


def _sincos_1d(pos, dim):
    half = dim // 2
    omega = 1.0 / (POS_BASE ** (jnp.arange(half, dtype=F32) / half))
    ang = pos.astype(F32)[:, None] * omega[None, :]
    return jnp.concatenate([jnp.sin(ang), jnp.cos(ang)], axis=-1)


def _sincos_2d(n_tokens, dim):
    rows = n_tokens // GRID_W
    emb_r = _sincos_1d(jnp.arange(rows), dim // 2)
    emb_c = _sincos_1d(jnp.arange(GRID_W), dim // 2)
    pe = jnp.concatenate([jnp.broadcast_to(emb_r[:, None, :], (rows, GRID_W, dim // 2)),
                          jnp.broadcast_to(emb_c[None, :, :], (rows, GRID_W, dim // 2))], axis=-1)
    return pe.reshape(rows * GRID_W, dim)


def kernel(x, c, ctx, c_ctx, ada_w, ada_b, norm_pre, norm_post, ffn_w1, ffn_w3, ffn_w2, lru_w_in, lru_conv_w, lru_conv_b, lru_wa, lru_ba, lru_wi, lru_bi, lru_lambda, lru_w_out, sgu_w_in, sgu_b_in, sgu_ln_g, sgu_ln_b, sgu_ws, sgu_bs, sgu_w_out):
    nb, seq, d = x.shape
    depth = ada_w.shape[0]
    assert depth == 2 and nb == SUBLANES

    pe = _sincos_2d(seq, d).astype(x.dtype)
    pad = jnp.zeros((2 * SUBLANES - nb - 1, d), F32)
    c_all = jnp.concatenate([c, c_ctx[None, :], pad], axis=0)
    mods = _ada_call(c_all, ada_w, ada_b).reshape(depth, 2 * SUBLANES, N_MOD, d)
    m_lat = [mods[i, :nb] for i in range(depth)]
    m_ctx = jnp.broadcast_to(mods[0, nb][None], (nb, N_MOD, d))

    def ffn(xs, pe_, mod, i, k, j, name):
        return _ffn_call(xs, pe_, mod, norm_pre[i, k], norm_post[i, k], ffn_w1[i, j], ffn_w3[i, j],
                         ffn_w2[i, j], k=k, name=name)

    def lru(xs, mod, h0f, h0b, name):
        return _lru_call(xs, mod.transpose(1, 0, 2), norm_pre[0, 1], norm_post[0, 1], lru_w_in[0],
                         lru_conv_w[0], lru_conv_b[0], lru_wa[0], lru_ba[0], lru_wi[0], lru_bi[0],
                         lru_lambda[0], lru_w_out[0], h0f, h0b, name=name)

    x_lat = ffn(x, pe, m_lat[0], 0, 0, 0, "l0_ffn0_lat")
    x_ctx = ffn(ctx, None, m_ctx, 0, 0, 0, "l0_ffn0_ctx")
    zeros = jnp.zeros((nb, d), F32)
    _, hc_f, hc_b = lru(x_ctx, m_ctx, zeros, zeros, "l0_lru_ctx")
    x_lat, _, _ = lru(x_lat, m_lat[0], hc_f, hc_b, "l0_lru_lat")
    x_lat = ffn(x_lat, None, m_lat[0], 0, 2, 1, "l0_ffn1_lat")

    x_lat = ffn(x_lat, None, m_lat[1], 1, 0, 0, "l1_ffn0_lat")
    x_lat = _sgu_call(x_lat, m_lat[1], norm_pre[1, 1], norm_post[1, 1], sgu_w_in[0], sgu_b_in[0],
                      sgu_ln_g[0], sgu_ln_b[0], sgu_ws[0], sgu_bs[0], sgu_w_out[0], name="l1_sgu_lat")
    return ffn(x_lat, None, m_lat[1], 1, 2, 1, "l1_ffn1_lat")
```
